```python
import jax, jax.numpy as jnp
from jax import lax
import numpy as np

D_MODEL = 1024
BATCH = 8
SEQ = 4096
DEPTH = 1

CHUNK = 64
POOL_WINDOWS = (2, 4, 8, 16)
N_POOL_GROUPS = len(POOL_WINDOWS)
POOL_WIDTH = D_MODEL // 2
POOL_GROUP_DIM = POOL_WIDTH // N_POOL_GROUPS
SSM_HEAD_DIM = 64
SSM_WIDTH = 3 * D_MODEL // 2
SSM_HEADS = SSM_WIDTH // SSM_HEAD_DIM
SSM_GROUPS = 4
SSM_STATE = 128
CONV_WIDTH = 4
MIX_WIDTH = POOL_WIDTH + SSM_WIDTH
CONV_CHANNELS = SSM_WIDTH + 2 * SSM_GROUPS * SSM_STATE
IN_PROJ_WIDTH = POOL_WIDTH + SSM_WIDTH + CONV_CHANNELS + SSM_HEADS
D_FF = 2816
FFN_RES_WEIGHT = 0.5
N_SUBLAYERS = 3
NORM_EPS = 1e-5

kernel_name = "hybrid_pool_ssd_macaron_adaln_block"


def rmsnorm(x, g):
    xf = x.astype(jnp.float32)
    y = xf * lax.rsqrt(jnp.mean(xf * xf, axis=-1, keepdims=True) + NORM_EPS)
    return (y * g.astype(jnp.float32)).astype(x.dtype)


def modulate(h, shift, scale):
    return h * (1.0 + scale[:, None, :]) + shift[:, None, :]


def swiglu(h, w_in, w_out):
    a, b = jnp.split(h @ w_in, 2, axis=-1)
    return (jax.nn.silu(a) * b) @ w_out


def pool_mixer(u, w_pool, pool_scale):
    b, s, _ = u.shape
    ug = u.reshape(b, s, N_POOL_GROUPS, POOL_GROUP_DIM).astype(jnp.float32)
    cs = jnp.cumsum(ug, axis=1)
    t = jnp.arange(s)
    means = []
    for gi, w in enumerate(POOL_WINDOWS):
        csg = cs[:, :, gi]
        lo = jnp.pad(csg, ((0, 0), (w, 0), (0, 0)))[:, :s]
        cnt = jnp.minimum(t + 1, w).astype(jnp.float32)[None, :, None]
        means.append((csg - lo) / cnt)
    pooled = (jnp.stack(means, axis=2) - ug).astype(u.dtype)
    mixed = jnp.einsum('bsgc,gcd->bsgd', pooled, w_pool)
    return mixed.reshape(b, s, POOL_WIDTH) * pool_scale


def causal_depthwise_conv(u, w, bias):
    ch = u.shape[-1]
    y = lax.conv_general_dilated(u, w[:, None, :].astype(u.dtype), window_strides=(1,),
                                 padding=[(CONV_WIDTH - 1, 0)],
                                 dimension_numbers=('NWC', 'WIO', 'NWC'),
                                 feature_group_count=ch)
    return y + bias


def segsum(a):
    L = a.shape[-1]
    cs = jnp.cumsum(a, axis=-1)
    diff = cs[..., :, None] - cs[..., None, :]
    mask = jnp.tril(jnp.ones((L, L), dtype=bool))
    return jnp.where(mask, diff, -jnp.inf)


def ssd_chunked(xs, dt, A, Bm, Cm):
    b, s, h, p = xs.shape
    g, n = Bm.shape[2], Bm.shape[3]
    r = h // g
    c = s // CHUNK
    xd = (xs.astype(jnp.float32) * dt[..., None]).reshape(b, c, CHUNK, g, r, p)
    a = (dt * A).reshape(b, c, CHUNK, g, r)
    a = jnp.moveaxis(a, 2, -1)
    Bc = Bm.astype(jnp.float32).reshape(b, c, CHUNK, g, n)
    Cc = Cm.astype(jnp.float32).reshape(b, c, CHUNK, g, n)
    a_cs = jnp.cumsum(a, axis=-1)
    scores = jnp.einsum('bclgn,bcmgn->bcglm', Cc, Bc)
    M = scores[:, :, :, None] * jnp.exp(segsum(a))
    y_diag = jnp.einsum('bcgrlm,bcmgrp->bclgrp', M, xd)
    decay_states = jnp.moveaxis(jnp.exp(a_cs[..., -1:] - a_cs), -1, 2)
    states = jnp.einsum('bcmgn,bcmgrp->bcgrpn', Bc, xd * decay_states[..., None])
    chunk_decay = jnp.exp(a_cs[..., -1])

    def step(carry, inp):
        st, dec = inp
        return carry * dec[..., None, None] + st, carry

    init = jnp.zeros((b, g, r, p, n), jnp.float32)
    _, prev = lax.scan(step, init, (jnp.moveaxis(states, 1, 0), jnp.moveaxis(chunk_decay, 1, 0)))
    prev = jnp.moveaxis(prev, 0, 1)
    decay_out = jnp.moveaxis(jnp.exp(a_cs), -1, 2)
    y_off = jnp.einsum('bclgn,bcgrpn->bclgrp', Cc, prev) * decay_out[..., None]
    return (y_diag + y_off).reshape(b, s, h, p)


def hybrid_mixer(h, w_in, w_pool, pool_scale, conv_w, conv_b, dt_bias, a_log, d_skip, ssm_norm_g, w_out):
    b, s, _ = h.shape
    proj = h @ w_in
    i0 = POOL_WIDTH
    i1 = i0 + SSM_WIDTH
    i2 = i1 + CONV_CHANNELS
    u_pool, z, xbc, dt_raw = jnp.split(proj, [i0, i1, i2], axis=-1)
    y_pool = pool_mixer(u_pool, w_pool, pool_scale)
    xbc = jax.nn.silu(causal_depthwise_conv(xbc, conv_w, conv_b))
    xs, Bm, Cm = jnp.split(xbc, [SSM_WIDTH, SSM_WIDTH + SSM_GROUPS * SSM_STATE], axis=-1)
    xs = xs.reshape(b, s, SSM_HEADS, SSM_HEAD_DIM)
    Bm = Bm.reshape(b, s, SSM_GROUPS, SSM_STATE)
    Cm = Cm.reshape(b, s, SSM_GROUPS, SSM_STATE)
    dt = jax.nn.softplus(dt_raw.astype(jnp.float32) + dt_bias.astype(jnp.float32))
    A = -jnp.exp(a_log.astype(jnp.float32))
    y = ssd_chunked(xs, dt, A, Bm, Cm) + d_skip.astype(jnp.float32)[:, None] * xs.astype(jnp.float32)
    y = y.reshape(b, s, SSM_WIDTH) * jax.nn.silu(z.astype(jnp.float32))
    yg = y.reshape(b, s, SSM_GROUPS, SSM_WIDTH // SSM_GROUPS)
    yg = yg * lax.rsqrt(jnp.mean(yg * yg, axis=-1, keepdims=True) + NORM_EPS)
    y_ssm = (yg.reshape(b, s, SSM_WIDTH) * ssm_norm_g.astype(jnp.float32)).astype(h.dtype)
    return jnp.concatenate([y_pool, y_ssm], axis=-1) @ w_out


def setup_inputs(seed: int = 0) -> dict:
    key = jax.random.key(seed)
    ks = jax.random.split(key, 24)
    f32 = jnp.float32
    nrm = lambda k, shape, scale: jax.random.normal(k, shape, f32) * scale
    dt0 = jnp.exp(jax.random.uniform(ks[14], (DEPTH, SSM_HEADS), f32, np.log(1e-3), np.log(1e-1)))
    return {
        "x": nrm(ks[0], (BATCH, SEQ, D_MODEL), 1.0),
        "c": nrm(ks[1], (BATCH, D_MODEL), 1.0),
        "w_ada": nrm(ks[2], (DEPTH, D_MODEL, N_SUBLAYERS * 3 * D_MODEL), 0.5 * D_MODEL ** -0.5),
        "b_ada": nrm(ks[3], (DEPTH, N_SUBLAYERS * 3 * D_MODEL), 0.01),
        "norm_g": 1.0 + nrm(ks[4], (DEPTH, N_SUBLAYERS, D_MODEL), 0.02),
        "ffn1_in": nrm(ks[5], (DEPTH, D_MODEL, 2 * D_FF), D_MODEL ** -0.5),
        "ffn1_out": nrm(ks[6], (DEPTH, D_FF, D_MODEL), D_FF ** -0.5),
        "w_in": nrm(ks[7], (DEPTH, D_MODEL, IN_PROJ_WIDTH), D_MODEL ** -0.5),
        "w_pool": nrm(ks[8], (DEPTH, N_POOL_GROUPS, POOL_GROUP_DIM, POOL_GROUP_DIM), POOL_GROUP_DIM ** -0.5),
        "pool_scale": 1.0 + nrm(ks[9], (DEPTH, POOL_WIDTH), 0.02),
        "conv_w": nrm(ks[10], (DEPTH, CONV_WIDTH, CONV_CHANNELS), CONV_WIDTH ** -0.5),
        "conv_b": nrm(ks[11], (DEPTH, CONV_CHANNELS), 0.01),
        "dt_bias": dt0 + jnp.log(-jnp.expm1(-dt0)),
        "a_log": jnp.log(jax.random.uniform(ks[12], (DEPTH, SSM_HEADS), f32, 1.0, 16.0)),
        "d_skip": 1.0 + nrm(ks[13], (DEPTH, SSM_HEADS), 0.1),
        "ssm_norm_g": 1.0 + nrm(ks[15], (DEPTH, SSM_WIDTH), 0.02),
        "w_out": nrm(ks[16], (DEPTH, MIX_WIDTH, D_MODEL), MIX_WIDTH ** -0.5),
        "ffn2_in": nrm(ks[17], (DEPTH, D_MODEL, 2 * D_FF), D_MODEL ** -0.5),
        "ffn2_out": nrm(ks[18], (DEPTH, D_FF, D_MODEL), D_FF ** -0.5),
        "final_g": 1.0 + nrm(ks[19], (D_MODEL,), 0.02),
    }


def reference(x, c, w_ada, b_ada, norm_g, ffn1_in, ffn1_out, w_in, w_pool, pool_scale, conv_w, conv_b,
              dt_bias, a_log, d_skip, ssm_norm_g, w_out, ffn2_in, ffn2_out, final_g):
    bsz = c.shape[0]
    for layer in range(DEPTH):
        mod = (jax.nn.silu(c) @ w_ada[layer] + b_ada[layer]).reshape(bsz, N_SUBLAYERS, 3, D_MODEL)
        shift, scale, gate = mod[:, :, 0], mod[:, :, 1], mod[:, :, 2]
        h = modulate(rmsnorm(x, norm_g[layer, 0]), shift[:, 0], scale[:, 0])
        x = x + FFN_RES_WEIGHT * gate[:, 0, None, :] * swiglu(h, ffn1_in[layer], ffn1_out[layer])
        h = modulate(rmsnorm(x, norm_g[layer, 1]), shift[:, 1], scale[:, 1])
        mix = hybrid_mixer(h, w_in[layer], w_pool[layer], pool_scale[layer], conv_w[layer], conv_b[layer],
                           dt_bias[layer], a_log[layer], d_skip[layer], ssm_norm_g[layer], w_out[layer])
        x = x + gate[:, 1, None, :] * mix
        h = modulate(rmsnorm(x, norm_g[layer, 2]), shift[:, 2], scale[:, 2])
        x = x + FFN_RES_WEIGHT * gate[:, 2, None, :] * swiglu(h, ffn2_in[layer], ffn2_out[layer])
    return rmsnorm(x, final_g)
```

```python
import functools

import jax
import jax.numpy as jnp
from jax import lax
from jax.experimental import pallas as pl
from jax.experimental.pallas import tpu as pltpu

CHUNK = 64
POOL_WINDOWS = (2, 4, 8, 16)
POOL_GROUP_DIM = 128
SSM_HEAD_DIM = 64
SSM_GROUPS = 4
SSM_STATE = 128
CONV_WIDTH = 4
FFN_RES_WEIGHT = 0.5
N_SUBLAYERS = 3
NORM_EPS = 1e-5

LANES = 128
SUBLANES = 8
VMEM_LIMIT_BYTES = 56 * 1024 * 1024

FFN_ROWS = 512
MIX_ROWS = 256
ADA_COLS = 2304

_NEG = -1e30


def _resident(shape):
  zeros = (0,) * len(shape)
  return pl.BlockSpec(shape, lambda i: zeros, pipeline_mode=pl.Buffered(1))


def _silu(v):
  return v * jax.nn.sigmoid(v)


def _rmsnorm(v, gain):
  ms = jnp.mean(v * v, axis=-1, keepdims=True)
  return v * lax.rsqrt(ms + NORM_EPS) * gain


def _adaln_kernel(c_ref, w_ref, b_ref, o_ref):
  a = _silu(c_ref[...]).astype(jnp.bfloat16)
  w = w_ref[...].astype(jnp.bfloat16)
  o_ref[...] = jnp.dot(a, w, preferred_element_type=jnp.float32) + b_ref[...]


def _adaln(c, w, b):
  bsz, d = c.shape
  n = w.shape[1]
  return pl.pallas_call(
      _adaln_kernel,
      grid=(n // ADA_COLS,),
      in_specs=[
          pl.BlockSpec((bsz, d), lambda j: (0, 0)),
          pl.BlockSpec((d, ADA_COLS), lambda j: (0, j)),
          pl.BlockSpec((1, ADA_COLS), lambda j: (0, j)),
      ],
      out_specs=pl.BlockSpec((bsz, ADA_COLS), lambda j: (0, j)),
      out_shape=jax.ShapeDtypeStruct((bsz, n), jnp.float32),
      compiler_params=pltpu.CompilerParams(
          dimension_semantics=("arbitrary",), vmem_limit_bytes=VMEM_LIMIT_BYTES),
      name="adaln",
  )(c, w, b.reshape(1, n))


def _ffn_kernel(x_ref, shift_ref, scale_ref, gate_ref, g_ref, win_ref, wout_ref,
                fg_ref, o_ref, *, d_ff, final_norm):
  x = x_ref[...]
  h = _rmsnorm(x, g_ref[...]) * (1.0 + scale_ref[...]) + shift_ref[...]
  h = h.astype(jnp.bfloat16)
  a = jnp.dot(h, win_ref[:, :d_ff], preferred_element_type=jnp.float32)
  b = jnp.dot(h, win_ref[:, d_ff:], preferred_element_type=jnp.float32)
  act = (_silu(a) * b).astype(jnp.bfloat16)
  y = jnp.dot(act, wout_ref[...], preferred_element_type=jnp.float32)
  out = x + (FFN_RES_WEIGHT * gate_ref[...]) * y
  if final_norm:
    out = _rmsnorm(out, fg_ref[...])
  o_ref[...] = out


def _ffn(x2, shift, scale, gate, g, w_in, w_out, final_g, *, seq, final_norm):
  t, d = x2.shape
  d_ff = w_out.shape[0]
  tiles_per_batch = seq // FFN_ROWS
  row = lambda i: (i, 0)
  per_batch = pl.BlockSpec((None, 1, d), lambda i: (i // tiles_per_batch, 0, 0))
  return pl.pallas_call(
      functools.partial(_ffn_kernel, d_ff=d_ff, final_norm=final_norm),
      grid=(t // FFN_ROWS,),
      in_specs=[
          pl.BlockSpec((FFN_ROWS, d), row),
          per_batch, per_batch, per_batch,
          _resident((1, d)),
          _resident((d, 2 * d_ff)),
          _resident((d_ff, d)),
          _resident((1, d)),
      ],
      out_specs=pl.BlockSpec((FFN_ROWS, d), row),
      out_shape=jax.ShapeDtypeStruct((t, d), jnp.float32),
      compiler_params=pltpu.CompilerParams(
          dimension_semantics=("arbitrary",), vmem_limit_bytes=VMEM_LIMIT_BYTES),
      name="ffn",
  )(x2, shift, scale, gate, g.reshape(1, d), w_in, w_out, final_g.reshape(1, d))


def _chunk_cumsum(a, pos):
  step = 1
  while step < CHUNK:
    shifted = pltpu.roll(a, step, axis=0)
    a = a + jnp.where(pos >= step, shifted, 0.0)
    step *= 2
  return a


def _expand_heads(v, head_lane, e3_ref):
  v = jnp.where(head_lane, v, 0.0)
  hi = v.astype(jnp.bfloat16).astype(jnp.float32)
  r1 = v - hi
  mid = r1.astype(jnp.bfloat16).astype(jnp.float32)
  lo = r1 - mid
  packed = hi + pltpu.roll(mid, 32, axis=1) + pltpu.roll(lo, 64, axis=1)
  return jnp.dot(packed.astype(jnp.bfloat16), e3_ref[...],
                 preferred_element_type=jnp.float32)


def _mixer_kernel(x_ref, shift_ref, scale_ref, gate_ref, g_ref, wall_ref,
                  convw_ref, convb_ref, wpool_ref, pscale_ref, dtb_ref, alog_ref,
                  dskip_ref, sg_ref, e3_ref, wout_ref, o_ref,
                  st_ref, xbc_buf, u_buf, xs_buf, b_buf, c_buf, accx_buf, dtx_buf,
                  y_buf, *, seq_tiles, n_heads, pool_w, ssm_w, conv_c):
  rows = x_ref.shape[0]
  s_idx = pl.program_id(0) % seq_tiles
  gw = ssm_w // SSM_GROUPS
  gn_state = SSM_GROUPS * SSM_STATE
  hist_c = SUBLANES
  hist_p = 2 * SUBLANES

  @pl.when(s_idx == 0)
  def _():
    st_ref[...] = jnp.zeros_like(st_ref)
    xbc_buf[0:hist_c, :] = jnp.zeros((hist_c, conv_c), jnp.float32)
    u_buf[0:hist_p, :] = jnp.zeros((hist_p, pool_w), jnp.float32)

  x = x_ref[...]
  h = _rmsnorm(x, g_ref[...]) * (1.0 + scale_ref[...]) + shift_ref[...]
  proj = jnp.dot(h.astype(jnp.bfloat16), wall_ref[...],
                 preferred_element_type=jnp.float32)
  o_z = pool_w
  o_xbc = o_z + ssm_w
  o_dt = o_xbc + conv_c
  u = proj[:, :o_z]
  z = proj[:, o_z:o_xbc]
  dt_raw = proj[:, o_dt:o_dt + LANES]

  xbc_buf[hist_c:hist_c + rows, :] = proj[:, o_xbc:o_dt]
  conv = convb_ref[...]
  for k in range(CONV_WIDTH):
    off = hist_c - (CONV_WIDTH - 1) + k
    conv = conv + convw_ref[k:k + 1, :] * xbc_buf[off:off + rows, :]
  act = _silu(conv)
  xs_buf[...] = act[:, :ssm_w]
  b_buf[...] = act[:, ssm_w:ssm_w + gn_state].astype(jnp.bfloat16)
  c_buf[...] = act[:, ssm_w + gn_state:].astype(jnp.bfloat16)
  xbc_buf[0:hist_c, :] = xbc_buf[rows:rows + hist_c, :]

  u_buf[hist_p:hist_p + rows, :] = u
  t_pos = s_idx * rows + lax.broadcasted_iota(jnp.int32, (rows, POOL_GROUP_DIM), 0)
  y_pool = []
  for gi, w in enumerate(POOL_WINDOWS):
    cols = slice(gi * POOL_GROUP_DIM, (gi + 1) * POOL_GROUP_DIM)
    tok = u_buf[hist_p:hist_p + rows, cols]
    wsum = tok
    for k in range(1, w):
      wsum = wsum + u_buf[hist_p - k:hist_p - k + rows, cols]
    cnt = jnp.minimum(t_pos + 1, w).astype(jnp.float32)
    pooled = (wsum / cnt - tok).astype(jnp.bfloat16)
    mixed = jnp.dot(pooled, wpool_ref[gi], preferred_element_type=jnp.float32)
    y_pool.append((mixed * pscale_ref[:, cols]).astype(jnp.bfloat16))
  u_buf[0:hist_p, :] = u_buf[rows:rows + hist_p, :]

  lane = lax.broadcasted_iota(jnp.int32, (rows, LANES), 1)
  pos = lax.broadcasted_iota(jnp.int32, (rows, LANES), 0) % CHUNK
  dt = jax.nn.softplus(dt_raw + dtb_ref[...])
  a_cs = _chunk_cumsum(dt * (-jnp.exp(alog_ref[...])), pos)
  head_lane = lane < n_heads
  accx_buf[...] = _expand_heads(a_cs, head_lane, e3_ref)
  dtx_buf[...] = _expand_heads(dt, head_lane, e3_ref)

  li = lax.broadcasted_iota(jnp.int32, (CHUNK, gw), 0)
  mi = lax.broadcasted_iota(jnp.int32, (CHUNK, gw), 1) % SSM_HEAD_DIM
  eye = li == mi
  causal = li >= mi
  pair_lane = lax.broadcasted_iota(jnp.int32, (CHUNK, LANES), 1)
  first_head = pair_lane < SSM_HEAD_DIM
  heads_per_group = gw // SSM_HEAD_DIM

  def chunk_body(ci, carry):
    r0 = pl.multiple_of(ci * CHUNK, CHUNK)
    crow = pl.ds(r0, CHUNK)
    for g in range(SSM_GROUPS):
      scol = slice(g * SSM_STATE, (g + 1) * SSM_STATE)
      gcol = slice(g * gw, (g + 1) * gw)
      cg = c_buf[crow, scol]
      bg = b_buf[crow, scol]
      ax = accx_buf[crow, gcol]
      dx = dtx_buf[crow, gcol]
      xs = xs_buf[crow, gcol]
      b_rep = jnp.concatenate([bg] * heads_per_group, axis=0)
      s6 = lax.dot_general(cg, b_rep, (((1,), (1,)), ((), ())),
                           preferred_element_type=jnp.float32)
      acs_m = jnp.sum(jnp.where(eye, ax, 0.0), axis=0, keepdims=True)
      dt_m = jnp.sum(jnp.where(eye, dx, 0.0), axis=0, keepdims=True)
      decay = jnp.exp(jnp.where(causal, ax - acs_m, _NEG))
      m6 = (s6 * decay * dt_m).astype(jnp.bfloat16)
      y_parts = []
      for j in range(gw // LANES):
        pcol = slice(j * LANES, (j + 1) * LANES)
        xp = xs[:, pcol]
        bd = jnp.concatenate(
            [jnp.where(first_head, xp, 0.0), jnp.where(first_head, 0.0, xp)],
            axis=0).astype(jnp.bfloat16)
        y_parts.append(jnp.dot(m6[:, pcol], bd, preferred_element_type=jnp.float32))
      y = jnp.concatenate(y_parts, axis=1)
      a_end = ax[CHUNK - 1:CHUNK, :]
      state = st_ref[g]
      y_off = jnp.dot(cg, state.astype(jnp.bfloat16),
                      preferred_element_type=jnp.float32)
      y = y + y_off * jnp.exp(ax) + dskip_ref[:, gcol] * xs
      y_buf[crow, gcol] = y
      xdd = (xs * (dx * jnp.exp(a_end - ax))).astype(jnp.bfloat16)
      new = lax.dot_general(bg, xdd, (((0,), (0,)), ((), ())),
                            preferred_element_type=jnp.float32)
      st_ref[g] = state * jnp.exp(a_end) + new
    return carry

  lax.fori_loop(0, rows // CHUNK, chunk_body, 0)

  y = y_buf[...] * _silu(z)
  mix = jnp.dot(jnp.concatenate(y_pool, axis=1), wout_ref[:pool_w, :],
                preferred_element_type=jnp.float32)
  for g in range(SSM_GROUPS):
    gcol = slice(g * gw, (g + 1) * gw)
    yg = _rmsnorm(y[:, gcol], sg_ref[:, gcol]).astype(jnp.bfloat16)
    mix = mix + jnp.dot(yg, wout_ref[pool_w + g * gw:pool_w + (g + 1) * gw, :],
                        preferred_element_type=jnp.float32)
  o_ref[...] = x + gate_ref[...] * mix


def _mixer(x2, shift, scale, gate, g, w_all, conv_w, conv_b, w_pool, pool_scale,
           dt_bias, a_log, d_skip_x, ssm_g, e3, w_out, *, seq, n_heads):
  t, d = x2.shape
  pool_w = pool_scale.shape[-1]
  ssm_w = ssm_g.shape[-1]
  conv_c = conv_w.shape[-1]
  gn_state = SSM_GROUPS * SSM_STATE
  seq_tiles = seq // MIX_ROWS
  row = lambda i: (i, 0)
  per_batch = pl.BlockSpec((None, 1, d), lambda i: (i // seq_tiles, 0, 0))
  f32 = jnp.float32
  return pl.pallas_call(
      functools.partial(_mixer_kernel, seq_tiles=seq_tiles, n_heads=n_heads,
                        pool_w=pool_w, ssm_w=ssm_w, conv_c=conv_c),
      grid=(t // MIX_ROWS,),
      in_specs=[
          pl.BlockSpec((MIX_ROWS, d), row),
          per_batch, per_batch, per_batch,
          _resident((1, d)),
          _resident(w_all.shape),
          _resident(conv_w.shape),
          _resident((1, conv_c)),
          _resident(w_pool.shape),
          _resident((1, pool_w)),
          _resident((1, LANES)),
          _resident((1, LANES)),
          _resident((1, ssm_w)),
          _resident((1, ssm_w)),
          _resident(e3.shape),
          _resident(w_out.shape),
      ],
      out_specs=pl.BlockSpec((MIX_ROWS, d), row),
      out_shape=jax.ShapeDtypeStruct((t, d), f32),
      scratch_shapes=[
          pltpu.VMEM((SSM_GROUPS, SSM_STATE, ssm_w // SSM_GROUPS), f32),
          pltpu.VMEM((SUBLANES + MIX_ROWS, conv_c), f32),
          pltpu.VMEM((2 * SUBLANES + MIX_ROWS, pool_w), f32),
          pltpu.VMEM((MIX_ROWS, ssm_w), f32),
          pltpu.VMEM((MIX_ROWS, gn_state), jnp.bfloat16),
          pltpu.VMEM((MIX_ROWS, gn_state), jnp.bfloat16),
          pltpu.VMEM((MIX_ROWS, ssm_w), f32),
          pltpu.VMEM((MIX_ROWS, ssm_w), f32),
          pltpu.VMEM((MIX_ROWS, ssm_w), f32),
      ],
      compiler_params=pltpu.CompilerParams(
          dimension_semantics=("arbitrary",), vmem_limit_bytes=VMEM_LIMIT_BYTES),
      name="mixer",
  )(x2, shift, scale, gate, g.reshape(1, d), w_all, conv_w,
    conv_b.reshape(1, conv_c), w_pool, pool_scale.reshape(1, pool_w), dt_bias,
    a_log, d_skip_x, ssm_g.reshape(1, ssm_w), e3, w_out)


def _pad_lanes(v):
  return jnp.pad(v, (0, LANES - v.shape[0])).reshape(1, LANES)


def _head_expander(n_heads):
  k = jnp.arange(LANES)[:, None]
  col_head = jnp.arange(n_heads * SSM_HEAD_DIM)[None, :] // SSM_HEAD_DIM
  hit = ((k % 32) == col_head) & (k < 96)
  return hit.astype(jnp.bfloat16)


def kernel(x, c, w_ada, b_ada, norm_g, ffn1_in, ffn1_out, w_in, w_pool, pool_scale,
           conv_w, conv_b, dt_bias, a_log, d_skip, ssm_norm_g, w_out, ffn2_in,
           ffn2_out, final_g):
  bsz, seq, d = x.shape
  depth = w_ada.shape[0]
  n_heads = dt_bias.shape[-1]
  pool_w = pool_scale.shape[-1]
  ssm_w = ssm_norm_g.shape[-1]
  conv_c = conv_w.shape[-1]
  assert n_heads <= 32 and n_heads * SSM_HEAD_DIM == ssm_w
  assert seq % FFN_ROWS == 0 and seq % MIX_ROWS == 0 and MIX_ROWS % CHUNK == 0
  bf16 = jnp.bfloat16
  e3 = _head_expander(n_heads)
  x2 = x.reshape(bsz * seq, d)
  for layer in range(depth):
    mod = _adaln(c, w_ada[layer], b_ada[layer]).reshape(bsz, N_SUBLAYERS, 3, 1, d)
    shift, scale, gate = mod[:, :, 0], mod[:, :, 1], mod[:, :, 2]
    x2 = _ffn(x2, shift[:, 0], scale[:, 0], gate[:, 0], norm_g[layer, 0],
              ffn1_in[layer].astype(bf16), ffn1_out[layer].astype(bf16), final_g,
              seq=seq, final_norm=False)
    wl = w_in[layer]
    w_all = jnp.concatenate(
        [wl[:, :pool_w + ssm_w + conv_c],
         jnp.pad(wl[:, pool_w + ssm_w + conv_c:], ((0, 0), (0, LANES - n_heads)))],
        axis=1).astype(bf16)
    x2 = _mixer(x2, shift[:, 1], scale[:, 1], gate[:, 1], norm_g[layer, 1], w_all,
                conv_w[layer], conv_b[layer], w_pool[layer].astype(bf16),
                pool_scale[layer], _pad_lanes(dt_bias[layer]), _pad_lanes(a_log[layer]),
                jnp.repeat(d_skip[layer], SSM_HEAD_DIM).reshape(1, ssm_w),
                ssm_norm_g[layer], e3, w_out[layer].astype(bf16),
                seq=seq, n_heads=n_heads)
    x2 = _ffn(x2, shift[:, 2], scale[:, 2], gate[:, 2], norm_g[layer, 2],
              ffn2_in[layer].astype(bf16), ffn2_out[layer].astype(bf16), final_g,
              seq=seq, final_norm=(layer == depth - 1))
  return x2.reshape(bsz, seq, d)
```

```python
import functools

import jax
import jax.numpy as jnp
from jax import lax
from jax.experimental import pallas as pl
from jax.experimental.pallas import tpu as pltpu

CHUNK = 64
POOL_WINDOWS = (2, 4, 8, 16)
POOL_GROUP_DIM = 128
SSM_HEAD_DIM = 64
SSM_GROUPS = 4
SSM_STATE = 128
CONV_WIDTH = 4
FFN_RES_WEIGHT = 0.5
N_SUBLAYERS = 3
NORM_EPS = 1e-5

LANES = 128
SUBLANES = 8
VMEM_LIMIT_BYTES = 56 * 1024 * 1024

FFN_ROWS = 512
MIX_ROWS = 256
ADA_COLS = 2304
CONV_COLS = 512
PROJ_COLS = 256
CONV_HIST = SUBLANES
POOL_HIST = 2 * SUBLANES

_NEG = -1e30


def _resident(shape):
  zeros = (0,) * len(shape)
  return pl.BlockSpec(shape, lambda i: zeros, pipeline_mode=pl.Buffered(1))


def _silu(v):
  half = 0.5 * v
  return half + half * jnp.tanh(half)


def _rmsnorm(v, gain):
  ms = jnp.mean(v * v, axis=-1, keepdims=True)
  return v * lax.rsqrt(ms + NORM_EPS) * gain


def _adaln_kernel(c_ref, w_ref, b_ref, o_ref):
  a = _silu(c_ref[...]).astype(jnp.bfloat16)
  w = w_ref[...].astype(jnp.bfloat16)
  o_ref[...] = jnp.dot(a, w, preferred_element_type=jnp.float32) + b_ref[...]


def _adaln(c, w, b):
  bsz, d = c.shape
  n = w.shape[1]
  return pl.pallas_call(
      _adaln_kernel,
      grid=(n // ADA_COLS,),
      in_specs=[
          pl.BlockSpec((bsz, d), lambda j: (0, 0)),
          pl.BlockSpec((d, ADA_COLS), lambda j: (0, j)),
          pl.BlockSpec((1, ADA_COLS), lambda j: (0, j)),
      ],
      out_specs=pl.BlockSpec((bsz, ADA_COLS), lambda j: (0, j)),
      out_shape=jax.ShapeDtypeStruct((bsz, n), jnp.float32),
      compiler_params=pltpu.CompilerParams(
          dimension_semantics=("arbitrary",), vmem_limit_bytes=VMEM_LIMIT_BYTES),
      name="adaln",
  )(c, w, b.reshape(1, n))


def _ffn_kernel(x_ref, shift_ref, scale_ref, gate_ref, g_ref, win_ref, wout_ref,
                fg_ref, o_ref, *, d_ff, final_norm):
  x = x_ref[...]
  h = _rmsnorm(x, g_ref[...]) * (1.0 + scale_ref[...]) + shift_ref[...]
  h = h.astype(jnp.bfloat16)
  a = jnp.dot(h, win_ref[:, :d_ff], preferred_element_type=jnp.float32)
  b = jnp.dot(h, win_ref[:, d_ff:], preferred_element_type=jnp.float32)
  act = (_silu(a) * b).astype(jnp.bfloat16)
  y = jnp.dot(act, wout_ref[...], preferred_element_type=jnp.float32)
  out = x + (FFN_RES_WEIGHT * gate_ref[...]) * y
  if final_norm:
    out = _rmsnorm(out, fg_ref[...])
  o_ref[...] = out


def _ffn(x2, shift, scale, gate, g, w_in, w_out, final_g, *, seq, final_norm):
  t, d = x2.shape
  d_ff = w_out.shape[0]
  tiles_per_batch = seq // FFN_ROWS
  row = lambda i: (i, 0)
  per_batch = pl.BlockSpec((None, 1, d), lambda i: (i // tiles_per_batch, 0, 0))
  return pl.pallas_call(
      functools.partial(_ffn_kernel, d_ff=d_ff, final_norm=final_norm),
      grid=(t // FFN_ROWS,),
      in_specs=[
          pl.BlockSpec((FFN_ROWS, d), row),
          per_batch, per_batch, per_batch,
          _resident((1, d)),
          _resident((d, 2 * d_ff)),
          _resident((d_ff, d)),
          _resident((1, d)),
      ],
      out_specs=pl.BlockSpec((FFN_ROWS, d), row),
      out_shape=jax.ShapeDtypeStruct((t, d), jnp.float32),
      compiler_params=pltpu.CompilerParams(
          dimension_semantics=("arbitrary",), vmem_limit_bytes=VMEM_LIMIT_BYTES),
      name="ffn",
  )(x2, shift, scale, gate, g.reshape(1, d), w_in, w_out, final_g.reshape(1, d))


def _chunk_cumsum(a, pos):
  step = 1
  while step < CHUNK:
    shifted = pltpu.roll(a, step, axis=0)
    a = a + jnp.where(pos >= step, shifted, 0.0)
    step *= 2
  return a


def _expand_heads(v, head_lane, e3_ref):
  v = jnp.where(head_lane, v, 0.0)
  hi = v.astype(jnp.bfloat16).astype(jnp.float32)
  r1 = v - hi
  mid = r1.astype(jnp.bfloat16).astype(jnp.float32)
  lo = r1 - mid
  packed = hi + pltpu.roll(mid, 32, axis=1) + pltpu.roll(lo, 64, axis=1)
  return jnp.dot(packed.astype(jnp.bfloat16), e3_ref[...],
                 preferred_element_type=jnp.float32)


def _mixer_step(proj_next, x_next, proj, x_prev, s_idx,
                x_ref, shift_ref, scale_ref, gate_ref, g_ref, wall_ref,
                convw_ref, convb_ref, wpool_ref, pscale_ref, dtb_ref, alog_ref,
                dskip_ref, sg_ref, e3_ref, wout_ref, o_ref,
                st_ref, conv_hist, pool_hist, xs_buf, b_buf, c_buf, accx_buf,
                dtx_buf, y_buf, *, n_heads, pool_w, ssm_w, conv_c):
  rows = x_ref.shape[0]
  gw = ssm_w // SSM_GROUPS
  gn_state = SSM_GROUPS * SSM_STATE
  o_z = pool_w
  o_xbc = o_z + ssm_w
  o_dt = o_xbc + conv_c

  x_in = x_ref[...]
  x_next[...] = x_in
  h = (_rmsnorm(x_in, g_ref[...]) * (1.0 + scale_ref[...]) + shift_ref[...]
       ).astype(jnp.bfloat16)
  n_proj = wall_ref.shape[1]
  pending = [(c0, min(c0 + PROJ_COLS, n_proj)) for c0 in range(0, n_proj, PROJ_COLS)]

  def project(n_blocks):
    for _ in range(min(n_blocks, len(pending))):
      c0, c1 = pending.pop(0)
      proj_next[:, c0:c1] = jnp.dot(h, wall_ref[:, c0:c1],
                                    preferred_element_type=jnp.float32)


  for cb in range(conv_c // CONV_COLS):
    cols = slice(cb * CONV_COLS, (cb + 1) * CONV_COLS)
    xe = jnp.concatenate(
        [conv_hist[:, cols],
         proj[:, o_xbc + cb * CONV_COLS:o_xbc + (cb + 1) * CONV_COLS]], axis=0)
    x2 = pltpu.roll(xe, 2, axis=0)
    inner = convw_ref[2:3, cols] * xe + convw_ref[0:1, cols] * x2
    conv = (convw_ref[3:4, cols] * xe + convw_ref[1:2, cols] * x2
            + pltpu.roll(inner, 1, axis=0) + convb_ref[:, cols])
    act = _silu(conv[CONV_HIST:, :])
    conv_hist[:, cols] = xe[rows:, :]
    for piece in range(CONV_COLS // LANES):
      c0 = cb * CONV_COLS + piece * LANES
      blk = act[:, piece * LANES:(piece + 1) * LANES]
      if c0 < ssm_w:
        xs_buf[:, c0:c0 + LANES] = blk
      elif c0 < ssm_w + gn_state:
        b_buf[:, c0 - ssm_w:c0 - ssm_w + LANES] = blk.astype(jnp.bfloat16)
      else:
        c1 = c0 - ssm_w - gn_state
        c_buf[:, c1:c1 + LANES] = blk.astype(jnp.bfloat16)
    project(2)

  t_pos = s_idx * rows + lax.broadcasted_iota(jnp.int32, (rows, POOL_GROUP_DIM), 0)
  y_pool = []
  for gi, w in enumerate(POOL_WINDOWS):
    cols = slice(gi * POOL_GROUP_DIM, (gi + 1) * POOL_GROUP_DIM)
    ue = jnp.concatenate([pool_hist[:, cols], proj[:, cols]], axis=0)
    pool_hist[:, cols] = ue[rows:, :]
    wsum = ue
    span = 1
    while span < w:
      wsum = wsum + pltpu.roll(wsum, span, axis=0)
      span *= 2
    tok = ue[POOL_HIST:, :]
    cnt = jnp.minimum(t_pos + 1, w).astype(jnp.float32)
    pooled = (wsum[POOL_HIST:, :] / cnt - tok).astype(jnp.bfloat16)
    mixed = jnp.dot(pooled, wpool_ref[gi], preferred_element_type=jnp.float32)
    y_pool.append((mixed * pscale_ref[:, cols]).astype(jnp.bfloat16))
  project(1)

  lane = lax.broadcasted_iota(jnp.int32, (rows, LANES), 1)
  pos = lax.broadcasted_iota(jnp.int32, (rows, LANES), 0) % CHUNK
  dt = jax.nn.softplus(proj[:, o_dt:o_dt + LANES] + dtb_ref[...])
  a_cs = _chunk_cumsum(dt * (-jnp.exp(alog_ref[...])), pos)
  head_lane = lane < n_heads
  accx_buf[...] = _expand_heads(a_cs, head_lane, e3_ref)
  dtx_buf[...] = _expand_heads(dt, head_lane, e3_ref)
  project(1)

  li = lax.broadcasted_iota(jnp.int32, (CHUNK, gw), 0)
  mi = lax.broadcasted_iota(jnp.int32, (CHUNK, gw), 1) % SSM_HEAD_DIM
  eye = li == mi
  causal = li >= mi
  pair_lane = lax.broadcasted_iota(jnp.int32, (CHUNK, LANES), 1)
  first_head = pair_lane < SSM_HEAD_DIM
  heads_per_group = gw // SSM_HEAD_DIM

  for ci in range(rows // CHUNK):
    crow = slice(ci * CHUNK, (ci + 1) * CHUNK)
    for g in range(SSM_GROUPS):
      scol = slice(g * SSM_STATE, (g + 1) * SSM_STATE)
      gcol = slice(g * gw, (g + 1) * gw)
      cg = c_buf[crow, scol]
      bg = b_buf[crow, scol]
      ax = accx_buf[crow, gcol]
      dx = dtx_buf[crow, gcol]
      xs = xs_buf[crow, gcol]
      b_rep = jnp.concatenate([bg] * heads_per_group, axis=0)
      s6 = lax.dot_general(cg, b_rep, (((1,), (1,)), ((), ())),
                           preferred_element_type=jnp.float32)
      acs_m = jnp.sum(jnp.where(eye, ax, 0.0), axis=0, keepdims=True)
      dt_m = jnp.sum(jnp.where(eye, dx, 0.0), axis=0, keepdims=True)
      decay = jnp.exp(jnp.where(causal, ax - acs_m, _NEG))
      m6 = (s6 * decay * dt_m).astype(jnp.bfloat16)
      y_parts = []
      for p in range(gw // LANES):
        pcol = slice(p * LANES, (p + 1) * LANES)
        xp = xs[:, pcol]
        bd = jnp.concatenate(
            [jnp.where(first_head, xp, 0.0), jnp.where(first_head, 0.0, xp)],
            axis=0).astype(jnp.bfloat16)
        y_parts.append(jnp.dot(m6[:, pcol], bd, preferred_element_type=jnp.float32))
      y = jnp.concatenate(y_parts, axis=1)
      a_end = ax[CHUNK - 1:CHUNK, :]
      state = st_ref[g]
      y_off = jnp.dot(cg, state.astype(jnp.bfloat16),
                      preferred_element_type=jnp.float32)
      y = y + y_off * jnp.exp(ax) + dskip_ref[:, gcol] * xs
      y_buf[crow, gcol] = y
      xdd = (xs * (dx * jnp.exp(a_end - ax))).astype(jnp.bfloat16)
      new = lax.dot_general(bg, xdd, (((0,), (0,)), ((), ())),
                            preferred_element_type=jnp.float32)
      st_ref[g] = state * jnp.exp(a_end) + new
      if g % 2 == 1:
        project(1)
  project(len(pending))

  mix = jnp.dot(jnp.concatenate(y_pool, axis=1), wout_ref[:pool_w, :],
                preferred_element_type=jnp.float32)
  for g in range(SSM_GROUPS):
    gcol = slice(g * gw, (g + 1) * gw)
    yg = y_buf[:, gcol] * _silu(proj[:, o_z + g * gw:o_z + (g + 1) * gw])
    yg = _rmsnorm(yg, sg_ref[:, gcol]).astype(jnp.bfloat16)
    mix = mix + jnp.dot(yg, wout_ref[pool_w + g * gw:pool_w + (g + 1) * gw, :],
                        preferred_element_type=jnp.float32)
  o_ref[...] = x_prev[...] + gate_ref[...] * mix


def _mixer_kernel(*refs, seq_tiles, **dims):
  n_in_out = 17
  io_refs = refs[:n_in_out]
  (st_ref, proj_even, proj_odd, x_even, x_odd, conv_hist, pool_hist,
   *work) = refs[n_in_out:]
  j = pl.program_id(0)
  s_idx = lax.rem(jnp.maximum(j - 1, 0), seq_tiles)

  @pl.when(j == 0)
  def _():
    proj_odd[...] = jnp.zeros_like(proj_odd)
    x_odd[...] = jnp.zeros_like(x_odd)

  @pl.when(jnp.logical_or(j == 0, s_idx == 0))
  def _():
    st_ref[...] = jnp.zeros_like(st_ref)
    conv_hist[...] = jnp.zeros_like(conv_hist)
    pool_hist[...] = jnp.zeros_like(pool_hist)

  for parity, slots in enumerate(((proj_even, x_even, proj_odd, x_odd),
                                  (proj_odd, x_odd, proj_even, x_even))):
    @pl.when(lax.rem(j, 2) == parity)
    def _():
      _mixer_step(*slots, s_idx, *io_refs, st_ref, conv_hist, pool_hist, *work,
                  **dims)


def _mixer(x2, shift, scale, gate, g, w_all, conv_w, conv_b, w_pool, pool_scale,
           dt_bias, a_log, d_skip_x, ssm_g, e3, w_out, *, seq, n_heads):
  t, d = x2.shape
  pool_w = pool_scale.shape[-1]
  ssm_w = ssm_g.shape[-1]
  conv_c = conv_w.shape[-1]
  gn_state = SSM_GROUPS * SSM_STATE
  seq_tiles = seq // MIX_ROWS
  n_tiles = t // MIX_ROWS
  f32 = jnp.float32
  tile_in = lambda j: jnp.minimum(j, n_tiles - 1)
  tile_out = lambda j: jnp.maximum(j - 1, 0)
  batch_in = pl.BlockSpec((None, 1, d), lambda j: (tile_in(j) // seq_tiles, 0, 0))
  batch_out = pl.BlockSpec((None, 1, d), lambda j: (tile_out(j) // seq_tiles, 0, 0))
  return pl.pallas_call(
      functools.partial(_mixer_kernel, seq_tiles=seq_tiles, n_heads=n_heads,
                        pool_w=pool_w, ssm_w=ssm_w, conv_c=conv_c),
      grid=(n_tiles + 1,),
      in_specs=[
          pl.BlockSpec((MIX_ROWS, d), lambda j: (tile_in(j), 0)),
          batch_in, batch_in, batch_out,
          _resident((1, d)),
          _resident(w_all.shape),
          _resident(conv_w.shape),
          _resident((1, conv_c)),
          _resident(w_pool.shape),
          _resident((1, pool_w)),
          _resident((1, LANES)),
          _resident((1, LANES)),
          _resident((1, ssm_w)),
          _resident((1, ssm_w)),
          _resident(e3.shape),
          _resident(w_out.shape),
      ],
      out_specs=pl.BlockSpec((MIX_ROWS, d), lambda j: (tile_out(j), 0)),
      out_shape=jax.ShapeDtypeStruct((t, d), f32),
      scratch_shapes=[
          pltpu.VMEM((SSM_GROUPS, SSM_STATE, ssm_w // SSM_GROUPS), f32),
          pltpu.VMEM((MIX_ROWS, w_all.shape[1]), f32),
          pltpu.VMEM((MIX_ROWS, w_all.shape[1]), f32),
          pltpu.VMEM((MIX_ROWS, d), f32),
          pltpu.VMEM((MIX_ROWS, d), f32),
          pltpu.VMEM((CONV_HIST, conv_c), f32),
          pltpu.VMEM((POOL_HIST, pool_w), f32),
          pltpu.VMEM((MIX_ROWS, ssm_w), f32),
          pltpu.VMEM((MIX_ROWS, gn_state), jnp.bfloat16),
          pltpu.VMEM((MIX_ROWS, gn_state), jnp.bfloat16),
          pltpu.VMEM((MIX_ROWS, ssm_w), f32),
          pltpu.VMEM((MIX_ROWS, ssm_w), f32),
          pltpu.VMEM((MIX_ROWS, ssm_w), f32),
      ],
      compiler_params=pltpu.CompilerParams(
          dimension_semantics=("arbitrary",), vmem_limit_bytes=VMEM_LIMIT_BYTES),
      name="mixer",
  )(x2, shift, scale, gate, g.reshape(1, d), w_all, conv_w,
    conv_b.reshape(1, conv_c), w_pool, pool_scale.reshape(1, pool_w), dt_bias,
    a_log, d_skip_x, ssm_g.reshape(1, ssm_w), e3, w_out)


def _pad_lanes(v):
  return jnp.pad(v, (0, LANES - v.shape[0])).reshape(1, LANES)


def _head_expander(n_heads):
  k = jnp.arange(LANES)[:, None]
  col_head = jnp.arange(n_heads * SSM_HEAD_DIM)[None, :] // SSM_HEAD_DIM
  hit = ((k % 32) == col_head) & (k < 96)
  return hit.astype(jnp.bfloat16)


def kernel(x, c, w_ada, b_ada, norm_g, ffn1_in, ffn1_out, w_in, w_pool, pool_scale,
           conv_w, conv_b, dt_bias, a_log, d_skip, ssm_norm_g, w_out, ffn2_in,
           ffn2_out, final_g):
  bsz, seq, d = x.shape
  depth = w_ada.shape[0]
  n_heads = dt_bias.shape[-1]
  pool_w = pool_scale.shape[-1]
  ssm_w = ssm_norm_g.shape[-1]
  conv_c = conv_w.shape[-1]
  assert n_heads <= 32 and n_heads * SSM_HEAD_DIM == ssm_w
  assert seq % FFN_ROWS == 0 and seq % MIX_ROWS == 0 and MIX_ROWS % CHUNK == 0
  assert conv_c % CONV_COLS == 0 and ssm_w % LANES == 0
  bf16 = jnp.bfloat16
  e3 = _head_expander(n_heads)
  x2 = x.reshape(bsz * seq, d)
  for layer in range(depth):
    mod = _adaln(c, w_ada[layer], b_ada[layer]).reshape(bsz, N_SUBLAYERS, 3, 1, d)
    shift, scale, gate = mod[:, :, 0], mod[:, :, 1], mod[:, :, 2]
    x2 = _ffn(x2, shift[:, 0], scale[:, 0], gate[:, 0], norm_g[layer, 0],
              ffn1_in[layer].astype(bf16), ffn1_out[layer].astype(bf16), final_g,
              seq=seq, final_norm=False)
    wl = w_in[layer]
    w_all = jnp.concatenate(
        [wl[:, :pool_w + ssm_w + conv_c],
         jnp.pad(wl[:, pool_w + ssm_w + conv_c:], ((0, 0), (0, LANES - n_heads)))],
        axis=1).astype(bf16)
    x2 = _mixer(x2, shift[:, 1], scale[:, 1], gate[:, 1], norm_g[layer, 1], w_all,
                conv_w[layer], conv_b[layer], w_pool[layer].astype(bf16),
                pool_scale[layer], _pad_lanes(dt_bias[layer]), _pad_lanes(a_log[layer]),
                jnp.repeat(d_skip[layer], SSM_HEAD_DIM).reshape(1, ssm_w),
                ssm_norm_g[layer], e3, w_out[layer].astype(bf16),
                seq=seq, n_heads=n_heads)
    x2 = _ffn(x2, shift[:, 2], scale[:, 2], gate[:, 2], norm_g[layer, 2],
              ffn2_in[layer].astype(bf16), ffn2_out[layer].astype(bf16), final_g,
              seq=seq, final_norm=(layer == depth - 1))
  return x2.reshape(bsz, seq, d)
```
